```python
import math
import jax, jax.numpy as jnp
from jax import lax
import numpy as np

D_MODEL = 2048
BATCH = 4
SEQ = 4096
DEPTH = 1
DEC_BATCH = 32
DEC_SEQ = 64
PAST_LEN = 2048

CHUNK = 64
Q_BLOCK = 128
D_CONV = D_MODEL // 2
CONV_WIDTH = 3
SB_HEADS = 8
SB_HEAD_DIM = 128
D_ATT = SB_HEADS * SB_HEAD_DIM
N_EXPERTS = 32
TOP_K = 4
D_FF = D_MODEL
SWIGLU_LIMIT = 7.0
SWIGLU_ALPHA = 1.702
LN_EPS = 1e-5
ALPHA = (2.0 * DEPTH) ** 0.25
BETA = (8.0 * DEPTH) ** -0.25
IN_WIDTHS = (D_CONV, D_CONV, D_CONV, D_ATT, D_ATT, D_ATT, D_MODEL, D_MODEL)
IN_SPLITS = tuple(int(s) for s in np.cumsum(IN_WIDTHS)[:-1])
IN_WIDTH = int(sum(IN_WIDTHS))

kernel_name = "hybrid_stickbreak_shortconv_moe_stream_step"


def layer_norm(x, g, b):
    xf = x.astype(jnp.float32)
    mu = jnp.mean(xf, axis=-1, keepdims=True)
    var = jnp.mean(jnp.square(xf - mu), axis=-1, keepdims=True)
    return ((xf - mu) * lax.rsqrt(var + LN_EPS) * g + b).astype(x.dtype)


def causal_conv3(xpad, w):
    n = xpad.shape[1] - (CONV_WIDTH - 1)
    return w[0] * xpad[:, :n] + w[1] * xpad[:, 1:n + 1] + w[2] * xpad[:, 2:]


def stick_breaking(q, k, v, q_start):
    bn, n, h, hd = q.shape
    s_len = k.shape[1]
    blk = min(Q_BLOCK, n)
    nb = n // blk
    scale = 1.0 / math.sqrt(hd)
    k_pos = jnp.arange(s_len)
    kf = k.astype(jnp.float32)
    vf = v.astype(jnp.float32)
    q_blocks = q.reshape(bn, nb, blk, h, hd).swapaxes(0, 1)
    q_pos = (q_start + jnp.arange(n)).reshape(nb, blk)

    def one_block(args):
        q_blk, qp = args
        z = jnp.einsum('bqhd,bshd->bhqs', q_blk.astype(jnp.float32), kf) * scale
        causal = k_pos[None, :] < qp[:, None]
        log_beta = jax.nn.log_sigmoid(z)
        log_keep = jnp.where(causal, jax.nn.log_sigmoid(-z), 0.0)
        rest = lax.cumsum(log_keep, axis=3, reverse=True) - log_keep
        w = jnp.where(causal, jnp.exp(log_beta + rest), 0.0)
        return jnp.einsum('bhqs,bshd->bqhd', w, vf)

    o = lax.map(one_block, (q_blocks, q_pos))
    return o.swapaxes(0, 1).reshape(bn, n, h, hd).astype(q.dtype)


def moe(h, w_router, b_router, w_gu, b_gu, w_dn, b_dn):
    logits = (h @ w_router + b_router).astype(jnp.float32)
    top_vals, top_idx = lax.top_k(logits, TOP_K)
    probs = jax.nn.softmax(top_vals, axis=-1)
    combine = jnp.sum(jax.nn.one_hot(top_idx, N_EXPERTS, dtype=jnp.float32) * probs[..., None], axis=1)

    def expert(acc, xs):
        w1, b1, w2, b2, g = xs
        gu = h @ w1 + b1
        gate = jnp.minimum(gu[:, :D_FF], SWIGLU_LIMIT)
        up = jnp.clip(gu[:, D_FF:], -SWIGLU_LIMIT, SWIGLU_LIMIT)
        out = ((up + 1.0) * gate * jax.nn.sigmoid(SWIGLU_ALPHA * gate)) @ w2 + b2
        return acc + g[:, None] * out.astype(jnp.float32), None

    acc0 = jnp.zeros(h.shape, jnp.float32)
    acc, _ = lax.scan(expert, acc0, (w_gu, b_gu, w_dn, b_dn, combine.T))
    return acc.astype(h.dtype)


def trunk_layer(u, c, conv_prev, k_past, v_past,
                w_ada, b_ada, w_in, conv_w, w_br_conv, w_br_att, w_out, ln1_g, ln1_b,
                w_router, b_router, w_gu, b_gu, w_dn, b_dn, ln2_g, ln2_b):
    bn, n, _ = u.shape
    past = k_past.shape[1]
    mod = (c @ w_ada + b_ada)[:, None, :]
    sh_t, sc_t, g_t, sh_f, sc_f, g_f = jnp.split(mod, 6, axis=-1)
    h = u * (1.0 + sc_t) + sh_t
    z = h @ w_in
    xc, bc, cc, q, k, v, ga, gb = jnp.split(z, IN_SPLITS, axis=-1)
    xin = cc * xc
    xpad = jnp.concatenate([conv_prev.astype(xin.dtype), xin], axis=1)
    yc = bc * causal_conv3(xpad, conv_w)
    q = q.reshape(bn, n, SB_HEADS, SB_HEAD_DIM)
    k = k.reshape(bn, n, SB_HEADS, SB_HEAD_DIM)
    v = v.reshape(bn, n, SB_HEADS, SB_HEAD_DIM)
    k_all = jnp.concatenate([k_past.astype(k.dtype), k], axis=1)
    v_all = jnp.concatenate([v_past.astype(v.dtype), v], axis=1)
    o = stick_breaking(q, k_all, v_all, past).reshape(bn, n, D_ATT)
    pc = yc @ w_br_conv
    pa = o @ w_br_att
    mix = (jax.nn.sigmoid(ga) * pc + jax.nn.sigmoid(gb) * pa) @ w_out
    u1 = layer_norm(ALPHA * u + (1.0 + g_t) * mix, ln1_g, ln1_b)
    h2 = u1 * (1.0 + sc_f) + sh_f
    f = moe(h2.reshape(bn * n, D_MODEL), w_router, b_router, w_gu, b_gu, w_dn, b_dn).reshape(bn, n, D_MODEL)
    u2 = layer_norm(ALPHA * u1 + (1.0 + g_f) * f, ln2_g, ln2_b)
    return u2, k, v, xpad[:, -(CONV_WIDTH - 1):]


def setup_inputs(seed: int = 0) -> dict:
    key = jax.random.key(seed)
    ks = jax.random.split(key, 32)
    f32 = jnp.float32

    def nrm(k, shape, s):
        return jax.random.normal(k, shape, f32) * s

    L, D = DEPTH, D_MODEL
    col_scale = jnp.concatenate([jnp.full((w,), s, f32) for w, s in zip(
        IN_WIDTHS, (BETA, 1.0, 1.0, 1.0, 1.0, BETA, 1.0, 1.0))])
    return {
        "x_prompt": nrm(ks[0], (BATCH, SEQ, D), 1.0),
        "x_sample": nrm(ks[1], (DEC_BATCH, DEC_SEQ, D), 1.0),
        "cache_k": nrm(ks[2], (L, DEC_BATCH, PAST_LEN, SB_HEADS, SB_HEAD_DIM), 1.0),
        "cache_v": nrm(ks[3], (L, DEC_BATCH, PAST_LEN, SB_HEADS, SB_HEAD_DIM), BETA),
        "cache_conv": nrm(ks[4], (L, DEC_BATCH, CONV_WIDTH - 1, D_CONV), BETA),
        "c_prompt": nrm(ks[5], (BATCH, D), 1.0),
        "c_sample": nrm(ks[6], (DEC_BATCH, D), 1.0),
        "ln0_g": 1.0 + nrm(ks[7], (D,), 0.02),
        "ln0_b": nrm(ks[8], (D,), 0.02),
        "w_ada": nrm(ks[9], (L, D, 6 * D), 0.1 * D ** -0.5),
        "b_ada": nrm(ks[10], (L, 6 * D), 0.02),
        "w_in": nrm(ks[11], (L, D, IN_WIDTH), D ** -0.5) * col_scale,
        "conv_w": nrm(ks[12], (L, CONV_WIDTH, D_CONV), CONV_WIDTH ** -0.5),
        "w_br_conv": nrm(ks[13], (L, D_CONV, D), D_CONV ** -0.5),
        "w_br_att": nrm(ks[14], (L, D_ATT, D), D_ATT ** -0.5),
        "w_out": nrm(ks[15], (L, D, D), BETA * D ** -0.5),
        "ln1_g": 1.0 + nrm(ks[16], (L, D), 0.02),
        "ln1_b": nrm(ks[17], (L, D), 0.02),
        "w_router": nrm(ks[18], (L, D, N_EXPERTS), D ** -0.5),
        "b_router": nrm(ks[19], (L, N_EXPERTS), 0.01),
        "w_gu": nrm(ks[20], (L, N_EXPERTS, D, 2 * D_FF), BETA * D ** -0.5),
        "b_gu": nrm(ks[21], (L, N_EXPERTS, 2 * D_FF), 0.02),
        "w_dn": nrm(ks[22], (L, N_EXPERTS, D_FF, D), BETA * D_FF ** -0.5),
        "b_dn": nrm(ks[23], (L, N_EXPERTS, D), 0.02),
        "ln2_g": 1.0 + nrm(ks[24], (L, D), 0.02),
        "ln2_b": nrm(ks[25], (L, D), 0.02),
    }


def reference(x_prompt, x_sample, cache_k, cache_v, cache_conv, c_prompt, c_sample,
              ln0_g, ln0_b, w_ada, b_ada, w_in, conv_w, w_br_conv, w_br_att, w_out,
              ln1_g, ln1_b, w_router, b_router, w_gu, b_gu, w_dn, b_dn, ln2_g, ln2_b):
    u_p = layer_norm(x_prompt, ln0_g, ln0_b)
    u_s = layer_norm(x_sample, ln0_g, ln0_b)
    bp = x_prompt.shape[0]
    kp_l, vp_l, cp_l, ks_l, vs_l, cs_l = [], [], [], [], [], []
    for l in range(DEPTH):
        layer_w = (w_ada[l], b_ada[l], w_in[l], conv_w[l], w_br_conv[l], w_br_att[l], w_out[l],
                   ln1_g[l], ln1_b[l], w_router[l], b_router[l], w_gu[l], b_gu[l], w_dn[l], b_dn[l],
                   ln2_g[l], ln2_b[l])
        conv0 = jnp.zeros((bp, CONV_WIDTH - 1, D_CONV), u_p.dtype)
        kv0 = jnp.zeros((bp, 0, SB_HEADS, SB_HEAD_DIM), u_p.dtype)
        u_p, k_p, v_p, cv_p = trunk_layer(u_p, c_prompt, conv0, kv0, kv0, *layer_w)
        u_s, k_s, v_s, cv_s = trunk_layer(u_s, c_sample, cache_conv[l], cache_k[l], cache_v[l], *layer_w)
        kp_l.append(k_p); vp_l.append(v_p); cp_l.append(cv_p)
        ks_l.append(k_s); vs_l.append(v_s); cs_l.append(cv_s)
    return (u_p, u_s, jnp.stack(kp_l), jnp.stack(vp_l), jnp.stack(cp_l),
            jnp.stack(ks_l), jnp.stack(vs_l), jnp.stack(cs_l))
```

```python
import functools
import math

import jax
import jax.numpy as jnp
from jax import lax
from jax.experimental import pallas as pl
from jax.experimental.pallas import tpu as pltpu

LN_EPS = 1e-5
SWIGLU_LIMIT = 7.0
SWIGLU_ALPHA = 1.702
TOP_K = 4
CONV_WIDTH = 3
LANES = 128
SUBLANES = 8
VMEM_LIMIT = 56 * 1024 * 1024

F32 = jnp.float32
BF16 = jnp.bfloat16


def _cparams(sem):
    return pltpu.CompilerParams(dimension_semantics=sem, vmem_limit_bytes=VMEM_LIMIT)


def _tile(n, pref):
    if n <= pref:
        return n
    t = pref - pref % SUBLANES
    while t > SUBLANES and n % t:
        t -= SUBLANES
    assert n % t == 0, (n, pref)
    return t


def _ln(x, g, b):
    mu = jnp.mean(x, axis=-1, keepdims=True)
    xc = x - mu
    var = jnp.mean(xc * xc, axis=-1, keepdims=True)
    return xc * lax.rsqrt(var + LN_EPS) * g + b


def _mod_kernel(c_ref, w_ref, b_ref, o_ref):
    o_ref[...] = jnp.dot(c_ref[...], w_ref[...], preferred_element_type=F32,
                         precision=lax.Precision.HIGHEST) + b_ref[...]


def _mod(c, w_ada, b_ada):
    bsz, d = c.shape
    w = w_ada.shape[1]
    tn = _tile(w, 1024)
    return pl.pallas_call(
        _mod_kernel,
        grid=(w // tn,),
        in_specs=[pl.BlockSpec((bsz, d), lambda j: (0, 0)),
                  pl.BlockSpec((d, tn), lambda j: (0, j)),
                  pl.BlockSpec((1, tn), lambda j: (0, j))],
        out_specs=pl.BlockSpec((bsz, tn), lambda j: (0, j)),
        out_shape=jax.ShapeDtypeStruct((bsz, w), F32),
        compiler_params=_cparams(("arbitrary",)),
        name="mod",
    )(c, w_ada, b_ada.reshape(1, w))


def _proj_kernel(x_ref, sh_ref, sc_ref, g_ref, b_ref, w_ref, z_ref, h_ref):
    @pl.when(pl.program_id(1) == 0)
    def _():
        u = _ln(x_ref[...], g_ref[...], b_ref[...])
        h = u * (1.0 + sc_ref[...]) + sh_ref[...]
        h_ref[...] = h.reshape(h_ref.shape).astype(BF16)

    z_ref[...] = jnp.dot(h_ref[...], w_ref[...], preferred_element_type=F32)


def _row_tiling(bsz, s, pref):
    if s >= pref:
        st = _tile(s, pref)
        return 1, st, s // st
    bb = max(1, min(bsz, pref // s))
    while bsz % bb:
        bb -= 1
    return bb, s, 1


def _proj(x3, mod3, ln_g, ln_b, w_bf, tm_pref=1024):
    bsz, s, d = x3.shape
    w = w_bf.shape[1]
    bb, st, tps = _row_tiling(bsz, s, tm_pref)
    tm = bb * st
    tn = _tile(w, 1024)
    n_rt = bsz * s // tm
    xmap = lambda i, j: (i // tps, i % tps, 0)
    return pl.pallas_call(
        _proj_kernel,
        grid=(n_rt, w // tn),
        in_specs=[pl.BlockSpec((bb, st, d), xmap),
                  pl.BlockSpec((bb, 1, d), lambda i, j: (i // tps, 0, 0)),
                  pl.BlockSpec((bb, 1, d), lambda i, j: (i // tps, 0, 1)),
                  pl.BlockSpec((1, d), lambda i, j: (0, 0)),
                  pl.BlockSpec((1, d), lambda i, j: (0, 0)),
                  pl.BlockSpec((d, tn), lambda i, j: (0, j))],
        out_specs=pl.BlockSpec((tm, tn), lambda i, j: (i, j)),
        out_shape=jax.ShapeDtypeStruct((bsz * s, w), F32),
        scratch_shapes=[pltpu.VMEM((tm, d), BF16)],
        compiler_params=_cparams(("parallel", "arbitrary")),
        name="proj",
    )(x3, mod3, mod3, ln_g, ln_b, w_bf)


def _strict_lower_ones(n):
    j = lax.broadcasted_iota(jnp.int32, (n, n), 0)
    s = lax.broadcasted_iota(jnp.int32, (n, n), 1)
    return jnp.where(j > s, 1.0, 0.0).astype(BF16)


def _sb_block(q, k, v, tri, carry, mask):
    z = lax.dot_general(q, k, (((1,), (1,)), ((), ())), preferred_element_type=F32)
    soft = jnp.log(1.0 + jnp.exp(-jnp.abs(z)))
    log_beta = jnp.minimum(z, 0.0) - soft
    log_keep = jnp.minimum(-z, 0.0) - soft
    if mask is not None:
        log_keep = jnp.where(mask, log_keep, 0.0)
    hi = log_keep.astype(BF16)
    lo = (log_keep - hi.astype(F32)).astype(BF16)
    rest = (jnp.dot(hi, tri, preferred_element_type=F32)
            + jnp.dot(lo, tri, preferred_element_type=F32))
    w = jnp.exp(log_beta + rest + carry)
    if mask is not None:
        w = jnp.where(mask, w, 0.0)
    out = jnp.dot(w.astype(BF16), v, preferred_element_type=F32)
    return out, carry + jnp.sum(log_keep, axis=1, keepdims=True)


def _attn_kernel(*refs, bq, bp, n_past, scale):
    if n_past:
        q_ref, kn_ref, vn_ref, kp_ref, vp_ref, o_ref = refs
    else:
        q_ref, kn_ref, vn_ref, o_ref = refs
    qi = pl.program_id(2)
    q = (q_ref[...] * scale).astype(BF16)
    tri_q = _strict_lower_ones(bq)

    def new_block(jb):
        r0 = pl.multiple_of(jb * bq, bq)
        return (kn_ref[pl.ds(r0, bq), :].astype(BF16), vn_ref[pl.ds(r0, bq), :].astype(BF16))

    row = lax.broadcasted_iota(jnp.int32, (bq, bq), 0)
    col = lax.broadcasted_iota(jnp.int32, (bq, bq), 1)
    kd, vd = new_block(qi)
    acc, carry = _sb_block(q, kd, vd, tri_q, jnp.zeros((bq, 1), F32), col < row)

    def new_body(i, st):
        acc, carry = st
        kb, vb = new_block(qi - 1 - i)
        out, carry = _sb_block(q, kb, vb, tri_q, carry, None)
        return acc + out, carry

    acc, carry = lax.fori_loop(0, qi, new_body, (acc, carry))

    if n_past:
        tri_p = _strict_lower_ones(bp)

        def past_body(i, st):
            acc, carry = st
            r0 = pl.multiple_of((n_past - 1 - i) * bp, bp)
            kb = kp_ref[pl.ds(r0, bp), :].astype(BF16)
            vb = vp_ref[pl.ds(r0, bp), :].astype(BF16)
            out, carry = _sb_block(q, kb, vb, tri_p, carry, None)
            return acc + out, carry

        acc, carry = lax.fori_loop(0, n_past, past_body, (acc, carry))

    o_ref[...] = acc.astype(o_ref.dtype)


def _attn(z, bsz, s, heads, hd, q_col, k_col, v_col, k_past, v_past, bq_pref=256, bp_pref=256):
    bq = _tile(s, bq_pref)
    nq = s // bq
    args = [z, z, z]
    in_specs = [pl.BlockSpec((bq, hd), lambda b, h, qi: (b * nq + qi, q_col + h)),
                pl.BlockSpec((s, hd), lambda b, h, qi: (b, k_col + h)),
                pl.BlockSpec((s, hd), lambda b, h, qi: (b, v_col + h))]
    n_past, bp = 0, 0
    if k_past is not None and k_past.shape[1] > 0:
        p = k_past.shape[1]
        bp = _tile(p, bp_pref)
        n_past = p // bp
        args += [k_past.reshape(bsz, p, heads * hd), v_past.reshape(bsz, p, heads * hd)]
        in_specs += [pl.BlockSpec((None, p, hd), lambda b, h, qi: (b, 0, h)),
                     pl.BlockSpec((None, p, hd), lambda b, h, qi: (b, 0, h))]
    kern = functools.partial(_attn_kernel, bq=bq, bp=bp, n_past=n_past, scale=1.0 / math.sqrt(hd))
    return pl.pallas_call(
        kern,
        grid=(bsz, heads, nq),
        in_specs=in_specs,
        out_specs=pl.BlockSpec((bq, hd), lambda b, h, qi: (b * nq + qi, h)),
        out_shape=jax.ShapeDtypeStruct((bsz * s, heads * hd), BF16),
        compiler_params=_cparams(("parallel", "parallel", "arbitrary")),
        name="attn",
    )(*args)


def _merge_kernel(x_ref, gt_ref, shf_ref, scf_ref, xc_ref, bc_ref, cc_ref, hxc_ref, hcc_ref,
                  cprev_ref, ga_ref, gb_ref, o_ref, convw_ref, wbc_ref, wba_ref, wout_ref,
                  g0_ref, b0_ref, g1_ref, b1_ref, wr_ref, br_ref,
                  u1_ref, h2_ref, idx_ref, prob_ref, cst_ref, pad_ref,
                  *, bb, st, tps, alpha, n_exp):
    tm = bb * st
    c = xc_ref.shape[-1]
    xin = (cc_ref[...] * xc_ref[...]).reshape(bb, st, c)
    prev = cprev_ref[...]
    if tps > 1:
        halo = (hcc_ref[...] * hxc_ref[...])[SUBLANES - 2:, :].reshape(1, 2, c)
        prev = jnp.where(pl.program_id(0) % tps == 0, prev, halo)
    pad_ref[:, SUBLANES - 2:SUBLANES, :] = prev
    pad_ref[:, SUBLANES:, :] = xin
    s1 = pad_ref[:, SUBLANES - 1:SUBLANES - 1 + st, :]
    s2 = pad_ref[:, SUBLANES - 2:SUBLANES - 2 + st, :]
    cw = convw_ref[...]
    conv = cw[0:1, :] * s2 + cw[1:2, :] * s1 + cw[2:3, :] * xin
    yc = (bc_ref[...].reshape(bb, st, c) * conv).reshape(tm, c)
    cst_ref[...] = pad_ref[:, SUBLANES + st - 2:SUBLANES + st, :]
    pc = jnp.dot(yc.astype(BF16), wbc_ref[...], preferred_element_type=F32)
    pa = jnp.dot(o_ref[...], wba_ref[...], preferred_element_type=F32)
    mixin = jax.nn.sigmoid(ga_ref[...]) * pc + jax.nn.sigmoid(gb_ref[...]) * pa
    mix = jnp.dot(mixin.astype(BF16), wout_ref[...], preferred_element_type=F32)
    d = mix.shape[-1]
    u = _ln(x_ref[...], g0_ref[...], b0_ref[...])
    r = alpha * u + (1.0 + gt_ref[...]) * mix.reshape(bb, st, d)
    u1 = _ln(r, g1_ref[...], b1_ref[...])
    h2 = (u1 * (1.0 + scf_ref[...]) + shf_ref[...]).reshape(tm, d)
    u1_ref[...] = u1.reshape(tm, d)
    h2_ref[...] = h2
    logits = jnp.dot(h2, wr_ref[...], preferred_element_type=F32,
                     precision=lax.Precision.HIGHEST) + br_ref[...]
    lane_e = lax.broadcasted_iota(jnp.int32, (tm, n_exp), 1).astype(F32)
    lane_o = lax.broadcasted_iota(jnp.int32, (tm, LANES), 1)
    idx_out = jnp.zeros((tm, LANES), jnp.int32)
    val_out = jnp.zeros((tm, LANES), F32)
    vals = []
    for k in range(TOP_K):
        m = jnp.max(logits, axis=1, keepdims=True)
        am = jnp.min(jnp.where(logits == m, lane_e, float(n_exp)), axis=1, keepdims=True)
        logits = jnp.where(lane_e == am, -jnp.inf, logits)
        idx_out = jnp.where(lane_o == k, am.astype(jnp.int32), idx_out)
        vals.append(m)
    exps = [jnp.exp(v - vals[0]) for v in vals]
    inv = 1.0 / functools.reduce(lambda a, b: a + b, exps)
    for k in range(TOP_K):
        val_out = jnp.where(lane_o == k, exps[k] * inv, val_out)
    idx_ref[...] = idx_out[:, :TOP_K]
    prob_ref[...] = val_out[:, :TOP_K]


def _merge(x3, mod3, z, o, conv_prev, conv_w, wbc, wba, wout, ln0_g, ln0_b, ln1_g, ln1_b,
           w_router, b_router, alpha, tm_pref=256):
    bsz, s, d = x3.shape
    c = conv_w.shape[-1]
    a = o.shape[-1]
    n_exp = w_router.shape[-1]
    bb, st, tps = _row_tiling(bsz, s, tm_pref)
    tm = bb * st
    n_rt = bsz * s // tm
    assert s >= CONV_WIDTH - 1 and st % SUBLANES == 0
    assert (3 * c + 3 * a) % d == 0
    g_col = (3 * c + 3 * a) // d
    tm8 = tm // SUBLANES
    xmap = lambda i: (i // tps, i % tps, 0)
    modspec = lambda k: pl.BlockSpec((bb, 1, d), lambda i: (i // tps, 0, k))
    const = lambda shape: pl.BlockSpec(shape, lambda i: (0,) * len(shape),
                                       pipeline_mode=pl.Buffered(1))
    halo = lambda col: pl.BlockSpec((SUBLANES, c), lambda i: (jnp.maximum(i * tm8 - 1, 0), col))
    n = bsz * s
    kern = functools.partial(_merge_kernel, bb=bb, st=st, tps=tps, alpha=alpha, n_exp=n_exp)
    return pl.pallas_call(
        kern,
        grid=(n_rt,),
        in_specs=[pl.BlockSpec((bb, st, d), xmap),
                  modspec(2), modspec(3), modspec(4),
                  pl.BlockSpec((tm, c), lambda i: (i, 0)),
                  pl.BlockSpec((tm, c), lambda i: (i, 1)),
                  pl.BlockSpec((tm, c), lambda i: (i, 2)),
                  halo(0), halo(2),
                  pl.BlockSpec((bb, CONV_WIDTH - 1, c), lambda i: (i // tps, 0, 0)),
                  pl.BlockSpec((tm, d), lambda i: (i, g_col)),
                  pl.BlockSpec((tm, d), lambda i: (i, g_col + 1)),
                  pl.BlockSpec((tm, a), lambda i: (i, 0)),
                  const((CONV_WIDTH, c)), const((c, d)), const((a, d)), const((d, d)),
                  const((1, d)), const((1, d)), const((1, d)), const((1, d)),
                  const((d, n_exp)), const((1, n_exp))],
        out_specs=[pl.BlockSpec((tm, d), lambda i: (i, 0)),
                   pl.BlockSpec((tm, d), lambda i: (i, 0)),
                   pl.BlockSpec((tm, TOP_K), lambda i: (i, 0)),
                   pl.BlockSpec((tm, TOP_K), lambda i: (i, 0)),
                   pl.BlockSpec((bb, CONV_WIDTH - 1, c), lambda i: (i // tps, 0, 0))],
        out_shape=[jax.ShapeDtypeStruct((n, d), F32),
                   jax.ShapeDtypeStruct((n, d), F32),
                   jax.ShapeDtypeStruct((n, TOP_K), jnp.int32),
                   jax.ShapeDtypeStruct((n, TOP_K), F32),
                   jax.ShapeDtypeStruct((bsz, CONV_WIDTH - 1, c), F32)],
        scratch_shapes=[pltpu.VMEM((bb, st + SUBLANES, c), F32)],
        compiler_params=_cparams(("arbitrary",)),
        name="merge",
    )(x3, mod3, mod3, mod3, z, z, z, z, z, conv_prev, z, z, o, conv_w, wbc, wba, wout,
      ln0_g, ln0_b, ln1_g, ln1_b, w_router, b_router)


def _expert_kernel(te_ref, nu_ref, tok_ref, h_hbm, wg_ref, wu_ref, wd_ref, bg_ref, bu_ref, bd_ref,
                   y_ref, xf_ref, xb_ref, sem):
    m = pl.program_id(0)
    j = pl.program_id(1)
    tm = xf_ref.shape[0]

    def row_copy(src_row, r):
        return pltpu.make_async_copy(h_hbm.at[pl.ds(src_row, 1)], xf_ref.at[pl.ds(r, 1)], sem.at[0])

    @pl.when(m < nu_ref[0])
    def _():
        @pl.when(j == 0)
        def _():
            def issue(r, carry):
                row_copy(tok_ref[0, 0, r], r).start()
                return carry

            def drain(r, carry):
                row_copy(0, r).wait()
                return carry

            lax.fori_loop(0, tm, issue, 0)
            lax.fori_loop(0, tm, drain, 0)
            xb_ref[...] = xf_ref[...].astype(BF16)

        x = xb_ref[...]
        gate = jnp.dot(x, wg_ref[0].astype(BF16), preferred_element_type=F32) + bg_ref[0]
        up = jnp.dot(x, wu_ref[0].astype(BF16), preferred_element_type=F32) + bu_ref[0]
        gate = jnp.minimum(gate, SWIGLU_LIMIT)
        up = jnp.clip(up, -SWIGLU_LIMIT, SWIGLU_LIMIT)
        act = (up + 1.0) * gate * jax.nn.sigmoid(SWIGLU_ALPHA * gate)
        part = jnp.dot(act.astype(BF16), wd_ref[0].astype(BF16), preferred_element_type=F32)

        @pl.when(j == 0)
        def _():
            y_ref[...] = part + bd_ref[0]

        @pl.when(j > 0)
        def _():
            y_ref[...] += part


def _experts(h2, tile_expert, n_used, row_token, w_gu, b_gu, w_dn, b_dn, tm, tf_pref=256):
    n_exp, d, f2 = w_gu.shape
    f = f2 // 2
    n_tiles = row_token.shape[0]
    tf = _tile(f, tf_pref)
    nj = f // tf

    def jj(m, j, nu):
        return jnp.where(m < nu[0], j, nj - 1)

    grid_spec = pltpu.PrefetchScalarGridSpec(
        num_scalar_prefetch=2,
        grid=(n_tiles, nj),
        in_specs=[pl.BlockSpec((1, 1, tm), lambda m, j, te, nu: (m, 0, 0), memory_space=pltpu.SMEM),
                  pl.BlockSpec(memory_space=pl.ANY),
                  pl.BlockSpec((1, d, tf), lambda m, j, te, nu: (te[m], 0, jj(m, j, nu))),
                  pl.BlockSpec((1, d, tf), lambda m, j, te, nu: (te[m], 0, nj + jj(m, j, nu))),
                  pl.BlockSpec((1, tf, d), lambda m, j, te, nu: (te[m], jj(m, j, nu), 0)),
                  pl.BlockSpec((1, 1, tf), lambda m, j, te, nu: (te[m], 0, jj(m, j, nu))),
                  pl.BlockSpec((1, 1, tf), lambda m, j, te, nu: (te[m], 0, nj + jj(m, j, nu))),
                  pl.BlockSpec((1, 1, d), lambda m, j, te, nu: (te[m], 0, 0))],
        out_specs=pl.BlockSpec((tm, d), lambda m, j, te, nu: (jnp.minimum(m, nu[0] - 1), 0)),
        scratch_shapes=[pltpu.VMEM((tm, d), F32), pltpu.VMEM((tm, d), BF16),
                        pltpu.SemaphoreType.DMA((1,))],
    )
    return pl.pallas_call(
        _expert_kernel,
        grid_spec=grid_spec,
        out_shape=jax.ShapeDtypeStruct((n_tiles * tm, d), F32),
        compiler_params=_cparams(("arbitrary", "arbitrary")),
        name="experts",
    )(tile_expert, n_used, row_token, h2, w_gu, w_gu, w_dn,
      b_gu.reshape(n_exp, 1, f2), b_gu.reshape(n_exp, 1, f2), b_dn.reshape(n_exp, 1, d))


def _combine_kernel(pos_ref, y_hbm, prob_ref, u1_ref, gf_ref, g2_ref, b2_ref, o_ref, buf_ref, sem,
                    *, bb, st, alpha):
    tt = bb * st
    d = u1_ref.shape[-1]

    def row_copy(src_row, k, r):
        return pltpu.make_async_copy(y_hbm.at[pl.ds(src_row, 1)], buf_ref.at[k, pl.ds(r, 1)],
                                     sem.at[0])

    def issue(r, carry):
        for k in range(TOP_K):
            row_copy(pos_ref[0, 0, TOP_K * r + k], k, r).start()
        return carry

    def drain(r, carry):
        for k in range(TOP_K):
            row_copy(0, k, r).wait()
        return carry

    lax.fori_loop(0, tt, issue, 0)
    lax.fori_loop(0, tt, drain, 0)
    prob = prob_ref[...]
    f = prob[:, 0:1] * buf_ref[0]
    for k in range(1, TOP_K):
        f = f + prob[:, k:k + 1] * buf_ref[k]
    r = alpha * u1_ref[...].reshape(bb, st, d) + (1.0 + gf_ref[...]) * f.reshape(bb, st, d)
    o_ref[...] = _ln(r, g2_ref[...], b2_ref[...])


def _combine(y, pos, prob, u1, mod3, ln2_g, ln2_b, bsz, s, alpha, tt_pref=128):
    n, d = u1.shape
    bb, st, tps = _row_tiling(bsz, s, tt_pref)
    tt = bb * st
    n_rt = n // tt
    kern = functools.partial(_combine_kernel, bb=bb, st=st, alpha=alpha)
    return pl.pallas_call(
        kern,
        grid=(n_rt,),
        in_specs=[pl.BlockSpec((1, 1, TOP_K * tt), lambda i: (i, 0, 0), memory_space=pltpu.SMEM),
                  pl.BlockSpec(memory_space=pl.ANY),
                  pl.BlockSpec((tt, TOP_K), lambda i: (i, 0)),
                  pl.BlockSpec((tt, d), lambda i: (i, 0)),
                  pl.BlockSpec((bb, 1, d), lambda i: (i // tps, 0, 5)),
                  pl.BlockSpec((1, d), lambda i: (0, 0)),
                  pl.BlockSpec((1, d), lambda i: (0, 0))],
        out_specs=pl.BlockSpec((bb, st, d), lambda i: (i // tps, i % tps, 0)),
        out_shape=jax.ShapeDtypeStruct((bsz, s, d), F32),
        scratch_shapes=[pltpu.VMEM((TOP_K, tt, d), F32), pltpu.SemaphoreType.DMA((1,))],
        compiler_params=_cparams(("arbitrary",)),
        name="combine",
    )(pos.reshape(n_rt, 1, TOP_K * tt), y, prob, u1, mod3, ln2_g, ln2_b)


def _routing(top_idx, n_exp, tm):
    n4 = top_idx.shape[0] * TOP_K
    n_tiles = n4 // tm + n_exp
    flat = top_idx.reshape(-1)
    onehot = (flat[:, None] == jnp.arange(n_exp, dtype=jnp.int32)[None, :]).astype(jnp.int32)
    csum = jnp.cumsum(onehot, axis=0)
    rank = jnp.sum(onehot * csum, axis=1) - 1
    counts = csum[-1]
    tiles_e = (counts + tm - 1) // tm
    tile_end = jnp.cumsum(tiles_e)
    seg_start = (tile_end - tiles_e) * tm
    pos = (seg_start[flat] + rank).astype(jnp.int32)
    n_used = tile_end[-1].astype(jnp.int32)
    row_token = jnp.zeros((n_tiles * tm,), jnp.int32).at[pos].set(
        jnp.arange(n4, dtype=jnp.int32) // TOP_K)
    m_ids = jnp.arange(n_tiles, dtype=jnp.int32)
    te = jnp.searchsorted(tile_end, jnp.minimum(m_ids, n_used - 1), side="right")
    te = jnp.minimum(te, n_exp - 1).astype(jnp.int32)
    return pos.reshape(-1, TOP_K), row_token.reshape(n_tiles, 1, tm), te, n_used.reshape(1)


def kernel(x_prompt, x_sample, cache_k, cache_v, cache_conv, c_prompt, c_sample, ln0_g, ln0_b, w_ada, b_ada, w_in, conv_w, w_br_conv, w_br_att, w_out, ln1_g, ln1_b, w_router, b_router, w_gu, b_gu, w_dn, b_dn, ln2_g, ln2_b):
    depth = w_ada.shape[0]
    bp, sp, d = x_prompt.shape
    bs, ss, _ = x_sample.shape
    heads, hd = cache_k.shape[-2:]
    a = heads * hd
    c = conv_w.shape[-1]
    n_exp = w_router.shape[-1]
    alpha = (2.0 * depth) ** 0.25
    assert c % hd == 0
    q_col, k_col, v_col = 3 * c // hd, (3 * c + a) // hd, (3 * c + 2 * a) // hd
    tm_exp = min(1024, (bp * sp + bs * ss) * TOP_K // n_exp)

    groups = [dict(x=x_prompt, c=c_prompt, b=bp, s=sp), dict(x=x_sample, c=c_sample, b=bs, s=ss)]
    c_all = jnp.concatenate([c_prompt, c_sample], axis=0)
    pad = (-c_all.shape[0]) % SUBLANES
    c_all = jnp.pad(c_all, ((0, pad), (0, 0)))
    row = lambda v: v.reshape(1, -1)
    outs = {k: [[], []] for k in ("k", "v", "conv")}
    xs = [g["x"] for g in groups]

    for l in range(depth):
        mod_all = _mod(c_all, w_ada[l], b_ada[l])
        mods = [mod_all[:bp].reshape(bp, 1, -1), mod_all[bp:bp + bs].reshape(bs, 1, -1)]
        w_in_bf = w_in[l].astype(BF16)
        wbc, wba, wout = w_br_conv[l].astype(BF16), w_br_att[l].astype(BF16), w_out[l].astype(BF16)
        assert depth == 1
        u1s, h2s, idxs, probs = [], [], [], []
        for gi, g in enumerate(groups):
            bsz, s = g["b"], g["s"]
            z = _proj(xs[gi], mods[gi], row(ln0_g), row(ln0_b), w_in_bf)
            if gi == 0:
                kpast = vpast = None
                cprev = jnp.zeros((bsz, CONV_WIDTH - 1, c), F32)
            else:
                kpast, vpast, cprev = cache_k[l], cache_v[l], cache_conv[l]
            o = _attn(z, bsz, s, heads, hd, q_col, k_col, v_col, kpast, vpast)
            u1, h2, idx, prob, cst = _merge(
                xs[gi], mods[gi], z, o, cprev, conv_w[l], wbc, wba, wout, row(ln0_g), row(ln0_b),
                row(ln1_g[l]), row(ln1_b[l]), w_router[l], row(b_router[l]), alpha)
            outs["k"][gi].append(z[:, 3 * c + a:3 * c + 2 * a].reshape(bsz, s, heads, hd))
            outs["v"][gi].append(z[:, 3 * c + 2 * a:3 * c + 3 * a].reshape(bsz, s, heads, hd))
            outs["conv"][gi].append(cst)
            u1s.append(u1); h2s.append(h2); idxs.append(idx); probs.append(prob)
        h2_all = jnp.concatenate(h2s, axis=0)
        pos, row_token, te, n_used = _routing(jnp.concatenate(idxs, axis=0), n_exp, tm_exp)
        y = _experts(h2_all, te, n_used, row_token, w_gu[l], b_gu[l], w_dn[l], b_dn[l], tm_exp)
        n0 = bp * sp
        xs = [_combine(y, pos[:n0], probs[0], u1s[0], mods[0], row(ln2_g[l]), row(ln2_b[l]),
                       bp, sp, alpha),
              _combine(y, pos[n0:], probs[1], u1s[1], mods[1], row(ln2_g[l]), row(ln2_b[l]),
                       bs, ss, alpha)]
    st = lambda lst: jnp.stack(lst)
    return (xs[0], xs[1], st(outs["k"][0]), st(outs["v"][0]), st(outs["conv"][0]),
            st(outs["k"][1]), st(outs["v"][1]), st(outs["conv"][1]))
```
